```python
import math
import jax
import jax.numpy as jnp
from jax import lax
import numpy as np

D_MODEL = 1024
BATCH = 2
SEQ = 8192
DEPTH = 2

HEAD_DIM = 64
BLOCK = 128
ROPE_THETA = 10000.0
RMS_EPS = 1e-6
FOX_HEADS = 8
FORGET_BIAS_INIT = 2.0
SWA_Q_HEADS = 8
SWA_KV_HEADS = 2
SWA_GROUP = SWA_Q_HEADS // SWA_KV_HEADS
SWA_WINDOW = 128
DIFF_HEADS = 4
DIL_HEADS = 8
DIL_PAIRS = ((128, 1), (512, 4), (2048, 16))
N_BRANCHES = 4
BRANCH_WIDTH = 512
IN_WIDTHS = (
    FOX_HEADS * HEAD_DIM, FOX_HEADS * HEAD_DIM, FOX_HEADS * HEAD_DIM, FOX_HEADS,
    SWA_Q_HEADS * HEAD_DIM, SWA_KV_HEADS * HEAD_DIM, SWA_KV_HEADS * HEAD_DIM,
    DIFF_HEADS * 2 * HEAD_DIM, DIFF_HEADS * 2 * HEAD_DIM, DIFF_HEADS * 2 * HEAD_DIM,
    DIL_HEADS * HEAD_DIM, DIL_HEADS * HEAD_DIM, DIL_HEADS * HEAD_DIM,
)
IN_COLS = sum(IN_WIDTHS)
N_EXPERTS = 32
TOP_K = 4
D_FF_EXPERT = D_MODEL
SWIGLU_ALPHA = 1.702
SWIGLU_LIMIT = 7.0
MOE_BLOCK = 128

kernel_name = 'hybrid_gated_fox_swa_diff_dilated_moe'


def rms_norm(x, g):
    xf = x.astype(jnp.float32)
    y = xf * lax.rsqrt(jnp.mean(xf * xf, axis=-1, keepdims=True) + RMS_EPS)
    return (y * g.astype(jnp.float32)).astype(x.dtype)


def rope_tables(seq_len):
    inv_freq = ROPE_THETA ** (-jnp.arange(0, HEAD_DIM, 2, dtype=jnp.float32) / HEAD_DIM)
    ang = jnp.arange(seq_len, dtype=jnp.float32)[:, None] * inv_freq[None, :]
    return jnp.cos(ang), jnp.sin(ang)


def apply_rope(x, cos, sin):
    half = x.shape[-1] // 2
    shape = (x.shape[1],) + (1,) * (x.ndim - 3) + (half,)
    c, s = cos.reshape(shape), sin.reshape(shape)
    xf = x.astype(jnp.float32)
    x1, x2 = xf[..., :half], xf[..., half:]
    return jnp.concatenate([x1 * c - x2 * s, x2 * c + x1 * s], axis=-1).astype(x.dtype)


def forgetting_attention(q, k, v, log_f):
    B, S, H, d = q.shape
    nb = S // BLOCK
    cum = jnp.cumsum(log_f, axis=1).transpose(0, 2, 1)
    q_blocks = q.reshape(B, nb, BLOCK, H, d).swapaxes(0, 1)
    c_blocks = cum.reshape(B, H, nb, BLOCK).transpose(2, 0, 1, 3)
    key_pos = jnp.arange(S)
    scale = d ** -0.5

    def block(args):
        i, qi, ci = args
        s = jnp.einsum('bqhd,bshd->bhqs', qi, k, preferred_element_type=jnp.float32) * scale
        s = s + ci[..., None] - cum[:, :, None, :]
        q_pos = i * BLOCK + jnp.arange(BLOCK)
        s = jnp.where(key_pos[None, :] <= q_pos[:, None], s, -jnp.inf)
        p = jax.nn.softmax(s, axis=-1)
        return jnp.einsum('bhqs,bshd->bqhd', p.astype(v.dtype), v,
                          preferred_element_type=jnp.float32).astype(v.dtype)

    out = lax.map(block, (jnp.arange(nb), q_blocks, c_blocks))
    return out.swapaxes(0, 1).reshape(B, S, H, d)


def differential_attention(q, k, v, lam):
    B, S, H, _, d = q.shape
    nb = S // BLOCK
    q_blocks = q.reshape(B, nb, BLOCK, H, 2, d).swapaxes(0, 1)
    key_pos = jnp.arange(S)
    scale = d ** -0.5

    def block(args):
        i, qi = args
        s = jnp.einsum('bqhmd,bshmd->bhmqs', qi, k, preferred_element_type=jnp.float32) * scale
        q_pos = i * BLOCK + jnp.arange(BLOCK)
        s = jnp.where(key_pos[None, :] <= q_pos[:, None], s, -jnp.inf)
        p = jax.nn.softmax(s, axis=-1)
        a = p[:, :, 0] - lam * p[:, :, 1]
        return jnp.einsum('bhqs,bshe->bqhe', a.astype(v.dtype), v,
                          preferred_element_type=jnp.float32).astype(v.dtype)

    out = lax.map(block, (jnp.arange(nb), q_blocks))
    return out.swapaxes(0, 1).reshape(B, S, H, v.shape[-1])


def banded_attention(q, k, v, max_dist, sink=None):
    N, L, K, G, d = q.shape
    nb = -(-L // BLOCK)
    pad = nb * BLOCK - L
    qb = jnp.pad(q, ((0, 0), (0, pad), (0, 0), (0, 0), (0, 0))).reshape(N, nb, BLOCK, K, G, d)
    kb = jnp.pad(k, ((0, 0), (BLOCK, pad), (0, 0), (0, 0))).reshape(N, nb + 1, BLOCK, K, d)
    vb = jnp.pad(v, ((0, 0), (BLOCK, pad), (0, 0), (0, 0))).reshape(N, nb + 1, BLOCK, K, v.shape[-1])
    kw = jnp.concatenate([kb[:, :-1], kb[:, 1:]], axis=2)
    vw = jnp.concatenate([vb[:, :-1], vb[:, 1:]], axis=2)
    s = jnp.einsum('nbqkgd,nbskd->nbkgqs', qb, kw, preferred_element_type=jnp.float32) * (d ** -0.5)
    dist = (BLOCK + jnp.arange(BLOCK))[:, None] - jnp.arange(2 * BLOCK)[None, :]
    key_pos = (jnp.arange(nb)[:, None] - 1) * BLOCK + jnp.arange(2 * BLOCK)[None, :]
    valid = (dist >= 0)[None] & (dist <= max_dist)[None] & (key_pos >= 0)[:, None, :]
    s = jnp.where(valid[None, :, None, None], s, -jnp.inf)
    m = jnp.max(s, axis=-1)
    if sink is not None:
        sk = sink.astype(jnp.float32)[:, :, None]
        m = jnp.maximum(m, sk)
    p = jnp.exp(s - m[..., None])
    denom = jnp.sum(p, axis=-1)
    if sink is not None:
        denom = denom + jnp.exp(sk - m)
    o = jnp.einsum('nbkgqs,nbske->nbqkge', p.astype(v.dtype), vw, preferred_element_type=jnp.float32)
    o = o / denom.transpose(0, 1, 4, 2, 3)[..., None]
    lse = (m + jnp.log(denom)).transpose(0, 1, 4, 2, 3)
    o = o.reshape(N, nb * BLOCK, K, G, v.shape[-1])[:, :L].astype(q.dtype)
    lse = lse.reshape(N, nb * BLOCK, K, G)[:, :L]
    return o, lse


def dilated_attention(q, k, v):
    B, S, H, d = q.shape
    outs, lses = [], []
    for window, rate in DIL_PAIRS:
        L = S // rate

        def by_residue(t):
            return t.reshape(B, L, rate, H, t.shape[-1]).swapaxes(1, 2).reshape(B * rate, L, H, t.shape[-1])

        o, lse = banded_attention(by_residue(q)[:, :, :, None], by_residue(k), by_residue(v), window // rate)
        outs.append(o[:, :, :, 0].reshape(B, rate, L, H, d).swapaxes(1, 2).reshape(B, S, H, d))
        lses.append(lse[..., 0].reshape(B, rate, L, H).swapaxes(1, 2).reshape(B, S, H))
    weights = jax.nn.softmax(jnp.stack(lses), axis=0)
    return jnp.einsum('pbsh,pbshd->bshd', weights, jnp.stack(outs).astype(jnp.float32)).astype(q.dtype)


def hybrid_mixer(h, w_in, b_forget, qk_gain, sinks, lambda_qk, g_diff, w_branch, w_gate, b_gate, w_out,
                 cos, sin, lambda_init):
    B, S, _ = h.shape
    split_at = np.cumsum(IN_WIDTHS)[:-1].tolist()
    qa, ka, va, fa, qb, kb, vb, qc, kc, vc, qd, kd, vd = jnp.split(h @ w_in, split_at, axis=-1)

    qa = rms_norm(qa.reshape(B, S, FOX_HEADS, HEAD_DIM), qk_gain[0, 0])
    ka = rms_norm(ka.reshape(B, S, FOX_HEADS, HEAD_DIM), qk_gain[0, 1])
    log_f = jax.nn.log_sigmoid((fa + b_forget).astype(jnp.float32))
    oa = forgetting_attention(qa, ka, va.reshape(B, S, FOX_HEADS, HEAD_DIM), log_f)

    qb = apply_rope(rms_norm(qb.reshape(B, S, SWA_KV_HEADS, SWA_GROUP, HEAD_DIM), qk_gain[1, 0]), cos, sin)
    kb = apply_rope(rms_norm(kb.reshape(B, S, SWA_KV_HEADS, HEAD_DIM), qk_gain[1, 1]), cos, sin)
    ob, _ = banded_attention(qb, kb, vb.reshape(B, S, SWA_KV_HEADS, HEAD_DIM), SWA_WINDOW - 1,
                             sinks.reshape(SWA_KV_HEADS, SWA_GROUP))

    qc = apply_rope(rms_norm(qc.reshape(B, S, DIFF_HEADS, 2, HEAD_DIM), qk_gain[2, 0]), cos, sin)
    kc = apply_rope(rms_norm(kc.reshape(B, S, DIFF_HEADS, 2, HEAD_DIM), qk_gain[2, 1]), cos, sin)
    lq1, lk1, lq2, lk2 = lambda_qk.astype(jnp.float32)
    lam = jnp.exp(jnp.sum(lq1 * lk1)) - jnp.exp(jnp.sum(lq2 * lk2)) + lambda_init
    oc = differential_attention(qc, kc, vc.reshape(B, S, DIFF_HEADS, 2 * HEAD_DIM), lam)
    oc = rms_norm(oc, g_diff) * (1.0 - lambda_init)

    qd = apply_rope(rms_norm(qd.reshape(B, S, DIL_HEADS, HEAD_DIM), qk_gain[3, 0]), cos, sin)
    kd = apply_rope(rms_norm(kd.reshape(B, S, DIL_HEADS, HEAD_DIM), qk_gain[3, 1]), cos, sin)
    od = dilated_attention(qd, kd, vd.reshape(B, S, DIL_HEADS, HEAD_DIM))

    branches = jnp.stack([o.reshape(B, S, BRANCH_WIDTH) for o in (oa, ob, oc, od)], axis=2)
    y = jnp.einsum('bsme,med->bsmd', branches, w_branch)
    gates = jax.nn.sigmoid((h @ w_gate + b_gate).astype(jnp.float32)).reshape(B, S, N_BRANCHES, D_MODEL)
    merged = jnp.sum(gates * y, axis=2).astype(h.dtype)
    return merged @ w_out


def moe_ffn(h, w_router, b_router, w1, b1, w2, b2):
    B, S, D = h.shape
    T = B * S
    TK = T * TOP_K
    ht = h.reshape(T, D)
    logits = (ht @ w_router + b_router).astype(jnp.float32)
    top_logit, top_idx = lax.top_k(logits, TOP_K)
    weights = jax.nn.softmax(top_logit, axis=-1)
    e_flat = top_idx.reshape(TK)
    order = jnp.argsort(e_flat)
    e_sorted = e_flat[order]
    tok_sorted = order // TOP_K
    w_sorted = weights.reshape(TK)[order]
    counts = jnp.bincount(e_flat, length=N_EXPERTS)
    padded = (counts + MOE_BLOCK - 1) // MOE_BLOCK * MOE_BLOCK
    pad_end = jnp.cumsum(padded)
    pad_start = pad_end - padded
    grp_start = jnp.cumsum(counts) - counts
    dest = pad_start[e_sorted] + jnp.arange(TK) - grp_start[e_sorted]
    n_blocks = -(-TK // MOE_BLOCK) + N_EXPERTS
    xs = jnp.zeros((n_blocks * MOE_BLOCK, D), h.dtype).at[dest].set(ht[tok_sorted])
    blk_expert = jnp.minimum(jnp.searchsorted(pad_end, jnp.arange(n_blocks) * MOE_BLOCK, side='right'),
                             N_EXPERTS - 1)

    def expert_block(args):
        xe, e = args
        u = jnp.einsum('md,df->mf', xe, w1[e], preferred_element_type=jnp.float32) + b1[e]
        x_glu = jnp.minimum(u[:, 0::2], SWIGLU_LIMIT)
        x_lin = jnp.clip(u[:, 1::2], -SWIGLU_LIMIT, SWIGLU_LIMIT)
        a = x_glu * jax.nn.sigmoid(SWIGLU_ALPHA * x_glu) * (x_lin + 1.0)
        return jnp.einsum('mf,fd->md', a.astype(xe.dtype), w2[e], preferred_element_type=jnp.float32) + b2[e]

    ys = lax.map(expert_block, (xs.reshape(n_blocks, MOE_BLOCK, D), blk_expert))
    y = ys.reshape(n_blocks * MOE_BLOCK, D)[dest].astype(jnp.float32) * w_sorted[:, None]
    return jax.ops.segment_sum(y, tok_sorted, num_segments=T).reshape(B, S, D).astype(h.dtype)


def setup_inputs(seed: int = 0) -> dict:
    key = jax.random.key(seed)
    ks = jax.random.split(key, 22)
    L, D, F, E = DEPTH, D_MODEL, D_FF_EXPERT, N_EXPERTS

    def nrm(k, shape, scale):
        return jax.random.normal(k, shape, jnp.float32) * scale

    return {
        'x': nrm(ks[0], (BATCH, SEQ, D), 1.0),
        'c': nrm(ks[1], (BATCH, D), 1.0),
        'w_ada': nrm(ks[2], (L, D, 6 * D), D ** -0.5),
        'b_ada': nrm(ks[3], (L, 6 * D), 0.01),
        'g_norm_mix': 1.0 + nrm(ks[4], (L, D), 0.02),
        'g_norm_ffn': 1.0 + nrm(ks[5], (L, D), 0.02),
        'w_in': nrm(ks[6], (L, D, IN_COLS), D ** -0.5),
        'b_forget': FORGET_BIAS_INIT + nrm(ks[7], (L, FOX_HEADS), 0.1),
        'qk_gain': 1.0 + nrm(ks[8], (L, 4, 2, HEAD_DIM), 0.02),
        'sinks': nrm(ks[9], (L, SWA_Q_HEADS), 0.1),
        'lambda_qk': nrm(ks[10], (L, 4, HEAD_DIM), 0.1),
        'g_diff': 1.0 + nrm(ks[11], (L, 2 * HEAD_DIM), 0.02),
        'w_branch': nrm(ks[12], (L, N_BRANCHES, BRANCH_WIDTH, D), BRANCH_WIDTH ** -0.5),
        'w_gate': nrm(ks[13], (L, D, N_BRANCHES * D), D ** -0.5),
        'b_gate': nrm(ks[14], (L, N_BRANCHES * D), 0.01),
        'w_out': nrm(ks[15], (L, D, D), D ** -0.5),
        'w_router': nrm(ks[16], (L, D, E), D ** -0.5),
        'b_router': nrm(ks[17], (L, E), 0.01),
        'w_mlp1': nrm(ks[18], (L, E, D, 2 * F), D ** -0.5),
        'b_mlp1': nrm(ks[19], (L, E, 2 * F), 0.01),
        'w_mlp2': nrm(ks[20], (L, E, F, D), F ** -0.5),
        'b_mlp2': nrm(ks[21], (L, E, D), 0.01),
    }


def reference(x, c, w_ada, b_ada, g_norm_mix, g_norm_ffn, w_in, b_forget, qk_gain, sinks, lambda_qk, g_diff,
              w_branch, w_gate, b_gate, w_out, w_router, b_router, w_mlp1, b_mlp1, w_mlp2, b_mlp2):
    cos, sin = rope_tables(x.shape[1])
    cond = jax.nn.silu(c)
    for l in range(DEPTH):
        lambda_init = 0.8 - 0.6 * math.exp(-0.3 * l)
        mod = (cond @ w_ada[l] + b_ada[l])[:, None, :]
        sh1, sc1, g1, sh2, sc2, g2 = jnp.split(mod, 6, axis=-1)
        h = rms_norm(x, g_norm_mix[l]) * (1.0 + sc1) + sh1
        x = x + g1 * hybrid_mixer(h, w_in[l], b_forget[l], qk_gain[l], sinks[l], lambda_qk[l], g_diff[l],
                                  w_branch[l], w_gate[l], b_gate[l], w_out[l], cos, sin, lambda_init)
        h = rms_norm(x, g_norm_ffn[l]) * (1.0 + sc2) + sh2
        x = x + g2 * moe_ffn(h, w_router[l], b_router[l], w_mlp1[l], b_mlp1[l], w_mlp2[l], b_mlp2[l])
    return x
```

```python
import functools
import math

import jax
import jax.numpy as jnp
import numpy as np
from jax import lax
from jax.experimental import pallas as pl
from jax.experimental.pallas import tpu as pltpu

_F32 = jnp.float32
_BF16 = jnp.bfloat16

HEAD_DIM = 64
LANES = 128
RMS_EPS = 1e-6
ROPE_THETA = 10000.0
FOX_HEADS = 8
SWA_Q_HEADS = 8
SWA_KV_HEADS = 2
SWA_WINDOW = 128
DIFF_HEADS = 4
DIL_HEADS = 8
DIL_PAIRS = ((128, 1), (512, 4), (2048, 16))
N_BRANCHES = 4
BRANCH_WIDTH = 512
N_EXPERTS = 32
TOP_K = 4
SWIGLU_ALPHA = 1.702
SWIGLU_LIMIT = 7.0
IN_WIDTHS = (512, 512, 512, 8, 512, 128, 128, 512, 512, 512, 512, 512, 512)

_NEG = -1e30
_VMEM_LIMIT = 48 * 1024 * 1024
_ATT_TILE = 512
_MOE_TILE = 256


def _cparams(*sem):
    return pltpu.CompilerParams(dimension_semantics=sem, vmem_limit_bytes=_VMEM_LIMIT)


def _ada_kernel(c_ref, w_ref, b_ref, o_ref):
    c = c_ref[...]
    cond = c * jax.nn.sigmoid(c)
    o_ref[...] = jnp.dot(cond, w_ref[...], preferred_element_type=_F32) + b_ref[...]


def _ada(c8, w, b):
    rows, d = c8.shape
    n = w.shape[1]
    tn = n // 4
    return pl.pallas_call(
        _ada_kernel,
        grid=(n // tn,),
        in_specs=[pl.BlockSpec((rows, d), lambda j: (0, 0)),
                  pl.BlockSpec((d, tn), lambda j: (0, j)),
                  pl.BlockSpec((1, tn), lambda j: (0, j))],
        out_specs=pl.BlockSpec((rows, tn), lambda j: (0, j)),
        out_shape=jax.ShapeDtypeStruct((rows, n), _F32),
        compiler_params=_cparams("parallel"),
        name="ada",
    )(c8, w, b.reshape(1, n))


def _norm_mod_body(x_ref, g_ref, sc_ref, sh_ref):
    x = x_ref[0]
    ms = jnp.mean(x * x, axis=-1, keepdims=True)
    y = x * lax.rsqrt(ms + RMS_EPS) * g_ref[...]
    return y * (1.0 + sc_ref[0]) + sh_ref[0]


def _norm_mod_kernel(x_ref, g_ref, sc_ref, sh_ref, o_ref):
    o_ref[0] = _norm_mod_body(x_ref, g_ref, sc_ref, sh_ref).astype(o_ref.dtype)


def _norm_mod_router_kernel(x_ref, g_ref, sc_ref, sh_ref, wr_ref, br_ref, o_ref, lg_ref):
    h = _norm_mod_body(x_ref, g_ref, sc_ref, sh_ref)
    o_ref[0] = h.astype(o_ref.dtype)
    lg_ref[0] = jnp.dot(h, wr_ref[...], preferred_element_type=_F32) + br_ref[...]


def _norm_mod(x, g, sc, sh, router=None):
    b, s, d = x.shape
    tm = min(512, s)
    in_specs = [pl.BlockSpec((1, tm, d), lambda bi, i: (bi, i, 0)),
                pl.BlockSpec((1, d), lambda bi, i: (0, 0)),
                pl.BlockSpec((1, 1, d), lambda bi, i: (bi, 0, 0)),
                pl.BlockSpec((1, 1, d), lambda bi, i: (bi, 0, 0))]
    args = [x, g.reshape(1, d), sc.reshape(b, 1, d), sh.reshape(b, 1, d)]
    h_spec = pl.BlockSpec((1, tm, d), lambda bi, i: (bi, i, 0))
    h_shape = jax.ShapeDtypeStruct((b, s, d), _BF16)
    if router is None:
        return pl.pallas_call(
            _norm_mod_kernel, grid=(b, s // tm), in_specs=in_specs, out_specs=h_spec,
            out_shape=h_shape, compiler_params=_cparams("parallel", "parallel"), name="norm_mod",
        )(*args)
    wr, br = router
    n = wr.shape[1]
    in_specs += [pl.BlockSpec((d, n), lambda bi, i: (0, 0)), pl.BlockSpec((1, n), lambda bi, i: (0, 0))]
    return pl.pallas_call(
        _norm_mod_router_kernel, grid=(b, s // tm), in_specs=in_specs,
        out_specs=[h_spec, pl.BlockSpec((1, tm, n), lambda bi, i: (bi, i, 0))],
        out_shape=[h_shape, jax.ShapeDtypeStruct((b, s, n), _F32)],
        compiler_params=_cparams("parallel", "parallel"), name="norm_mod_router",
    )(*args, wr, br)


def _proj_kernel(mode, tn, h_ref, w_ref, *rest):
    o_ref = rest[-1]
    acc = jnp.dot(h_ref[...], w_ref[...], preferred_element_type=_F32)
    if mode == "plain":
        o_ref[...] = acc.astype(o_ref.dtype)
        return
    gain_ref, ones_ref = rest[0], rest[1]
    for c in range(tn // 256):
        cols = slice(c * 256, (c + 1) * 256)
        y = acc[:, cols]
        ss = jnp.dot((y * y).astype(_BF16), ones_ref[...], preferred_element_type=_F32)
        y = y * lax.rsqrt(ss * (1.0 / HEAD_DIM) + RMS_EPS) * gain_ref[:, cols]
        if mode == "norm_rope":
            cos_ref, sin_ref = rest[2], rest[3]
            halves = []
            for hh in range(2):
                yh = y[:, hh * LANES:(hh + 1) * LANES]
                lane = lax.broadcasted_iota(jnp.int32, yh.shape, 1)
                first = (lane & 32) == 0
                partner = jnp.where(first, pltpu.roll(yh, LANES - 32, 1), pltpu.roll(yh, 32, 1))
                halves.append(yh * cos_ref[...] + partner * sin_ref[...])
            y = jnp.concatenate(halves, axis=1)
        o_ref[:, cols] = y.astype(o_ref.dtype)


def _proj(h2d, w, mode, seq, *, gain=None, ones=None, cos=None, sin=None, tm=1024, tn=512,
          out_dtype=_BF16):
    t, d = h2d.shape
    n = w.shape[1]
    tm = min(tm, seq)
    tn = min(tn, n)
    assert t % tm == 0 and n % tn == 0 and seq % tm == 0
    in_specs = [pl.BlockSpec((tm, d), lambda i, j: (i, 0)),
                pl.BlockSpec((d, tn), lambda i, j: (0, j))]
    args = [h2d, w]
    if mode != "plain":
        in_specs += [pl.BlockSpec((1, tn), lambda i, j: (0, j)),
                     pl.BlockSpec((256, 256), lambda i, j: (0, 0))]
        args += [gain, ones]
    if mode == "norm_rope":
        nst = seq // tm
        in_specs += [pl.BlockSpec((tm, LANES), lambda i, j: (i % nst, 0)),
                     pl.BlockSpec((tm, LANES), lambda i, j: (i % nst, 0))]
        args += [cos, sin]
    return pl.pallas_call(
        functools.partial(_proj_kernel, mode, tn),
        grid=(t // tm, n // tn),
        in_specs=in_specs,
        out_specs=pl.BlockSpec((tm, tn), lambda i, j: (i, j)),
        out_shape=jax.ShapeDtypeStruct((t, n), out_dtype),
        compiler_params=_cparams("parallel", "parallel"),
        name="proj_" + mode,
    )(*args)


def _attn_kernel(cfg, *refs):
    tq, tk, nback, causal_full, fox, sinks, diff = cfg
    it = iter(refs)
    q_ref, k_ref, v_ref = next(it), next(it), next(it)
    ccol_ref = crow_ref = scal_ref = g_ref = None
    if fox:
        ccol_ref, crow_ref = next(it), next(it)
    bias_ref = next(it)
    if sinks or diff:
        scal_ref = next(it)
    if diff:
        g_ref = next(it)
    o_ref = next(it)
    m_ref, l_ref, acc_ref = next(it), next(it), next(it)
    pr = pl.program_id(1)
    i = pl.program_id(2)

    q2 = q_ref[0].astype(_F32)
    lane = lax.broadcasted_iota(jnp.int32, q2.shape, 1)
    qh = (jnp.where(lane < HEAD_DIM, q2, 0.0).astype(_BF16),
          jnp.where(lane >= HEAD_DIM, q2, 0.0).astype(_BF16))
    m_ref[...] = jnp.full(m_ref.shape, _NEG, _F32)
    l_ref[...] = jnp.zeros(l_ref.shape, _F32)
    acc_ref[...] = jnp.zeros(acc_ref.shape, _F32)

    def tile(j, bias):
        ks = pl.multiple_of(j * tk, tk)
        k2 = k_ref[0, pl.ds(ks, tk), :]
        v2 = v_ref[0, pl.ds(ks, tk), :]
        for h in range(2):
            s = lax.dot_general(qh[h], k2, (((1,), (1,)), ((), ())), preferred_element_type=_F32)
            if fox:
                s = s + ccol_ref[0, h] - crow_ref[0, h, :, pl.ds(ks, tk)]
            if bias is not None:
                s = s + bias
            m_prev = m_ref[h]
            m_new = jnp.maximum(m_prev, jnp.max(s, axis=1, keepdims=True))
            p = jnp.exp(s - m_new)
            alpha = jnp.exp(m_prev - m_new)
            l_ref[h] = alpha * l_ref[h] + jnp.sum(p, axis=1, keepdims=True)
            acc_ref[h] = alpha * acc_ref[h] + jnp.dot(p.astype(_BF16), v2, preferred_element_type=_F32)
            m_ref[h] = m_new

    tile(i, bias_ref[0])
    if causal_full:
        def body(j, carry):
            tile(j, None)
            return carry
        lax.fori_loop(0, i, body, 0)
    else:
        for d in range(1, nback + 1):
            @pl.when(i >= d)
            def _(d=d):
                tile(i - d, bias_ref[d])

    outs = []
    for h in range(2):
        m, l, acc = m_ref[h], l_ref[h], acc_ref[h]
        if sinks:
            sk = scal_ref[2 * pr + h]
            m_f = jnp.maximum(m, sk)
            w = jnp.exp(m - m_f)
            l = l * w + jnp.exp(sk - m_f)
            acc = acc * w
        outs.append(acc / l)
    if diff:
        o = outs[0] - scal_ref[0] * outs[1]
        ms = jnp.mean(o * o, axis=1, keepdims=True)
        o = o * lax.rsqrt(ms + RMS_EPS) * g_ref[...]
    else:
        o = jnp.where(lane < HEAD_DIM, outs[0], outs[1])
    o_ref[0] = o.astype(o_ref.dtype)


def _attention(q, qmap, k, kmap, v, vmap, bias, n_blocks, *, causal_full=False, cum=None,
               sinks=None, lam=None, g=None):
    b, s, _ = q.shape
    tq = min(_ATT_TILE, s)
    nq = s // tq
    nback = bias.shape[0] - 1
    fox = cum is not None
    cfg = (tq, tq, nback, causal_full, fox, sinks is not None, lam is not None)
    in_specs = [pl.BlockSpec((1, tq, LANES), lambda bi, p, i: (bi, i, qmap(p))),
                pl.BlockSpec((1, s, LANES), lambda bi, p, i: (bi, 0, kmap(p))),
                pl.BlockSpec((1, s, LANES), lambda bi, p, i: (bi, 0, vmap(p)))]
    args = [q, k, v]
    if fox:
        ccol, crow = cum
        in_specs += [pl.BlockSpec((1, 2, tq, 1), lambda bi, p, i: (bi, p, i, 0)),
                     pl.BlockSpec((1, 2, 1, s), lambda bi, p, i: (bi, p, 0, 0))]
        args += [ccol, crow]
    in_specs.append(pl.BlockSpec(bias.shape, lambda bi, p, i: (0, 0, 0)))
    args.append(bias)
    if sinks is not None:
        in_specs.append(pl.BlockSpec(memory_space=pltpu.SMEM))
        args.append(sinks)
    if lam is not None:
        in_specs += [pl.BlockSpec(memory_space=pltpu.SMEM), pl.BlockSpec((1, LANES), lambda bi, p, i: (0, 0))]
        args += [lam, g]
    return pl.pallas_call(
        functools.partial(_attn_kernel, cfg),
        grid=(b, n_blocks, nq),
        in_specs=in_specs,
        out_specs=pl.BlockSpec((1, tq, LANES), lambda bi, p, i: (bi, i, p)),
        out_shape=jax.ShapeDtypeStruct((b, s, n_blocks * LANES), _BF16),
        scratch_shapes=[pltpu.VMEM((2, tq, 1), _F32), pltpu.VMEM((2, tq, 1), _F32),
                        pltpu.VMEM((2, tq, LANES), _F32)],
        compiler_params=_cparams("parallel", "parallel", "arbitrary"),
        name="attn",
    )(*args)


def _bias_tables(tq, nback, fn):
    r = np.arange(tq)[:, None]
    c = np.arange(tq)[None, :]
    tabs = []
    for d in range(nback + 1):
        mult = fn(d * tq + r - c).astype(np.float64)
        tabs.append(np.where(mult > 0, np.log(np.maximum(mult, 1.0)), _NEG))
    return jnp.asarray(np.stack(tabs), _F32)


def _mult_causal(delta):
    return (delta >= 0).astype(np.int64)


def _mult_swa(delta):
    return ((delta >= 0) & (delta <= SWA_WINDOW - 1)).astype(np.int64)


def _mult_dilated(delta):
    mult = np.zeros_like(delta)
    for window, rate in DIL_PAIRS:
        mult = mult + ((delta >= 0) & (delta % rate == 0) & (delta <= window)).astype(np.int64)
    return mult


def _merge_kernel(h_ref, oa_ref, ob_ref, oc_ref, od_ref, wg0, wg1, wg2, wg3, bg0, bg1, bg2, bg3,
                  wb_ref, o_ref):
    h = h_ref[...]
    acc = None
    for m, (o_m, wg, bg) in enumerate(((oa_ref, wg0, bg0), (ob_ref, wg1, bg1),
                                       (oc_ref, wg2, bg2), (od_ref, wg3, bg3))):
        gate = jax.nn.sigmoid(jnp.dot(h, wg[...], preferred_element_type=_F32) + bg[...])
        y = jnp.dot(o_m[...], wb_ref[m], preferred_element_type=_F32)
        acc = gate * y if acc is None else acc + gate * y
    o_ref[...] = acc.astype(o_ref.dtype)


def _merge(h2d, outs, w_gate, b_gate, w_branch, seq, tm=1024, tn=512):
    t, d = h2d.shape
    tm = min(tm, seq)
    nj = d // tn
    bw = w_branch.shape[1]
    act_spec = pl.BlockSpec((tm, bw), lambda i, j: (i, 0))
    in_specs = [pl.BlockSpec((tm, d), lambda i, j: (i, 0))] + [act_spec] * 4
    in_specs += [pl.BlockSpec((d, tn), functools.partial(lambda m, i, j: (0, m * nj + j), m))
                 for m in range(N_BRANCHES)]
    in_specs += [pl.BlockSpec((1, tn), functools.partial(lambda m, i, j: (0, m * nj + j), m))
                 for m in range(N_BRANCHES)]
    in_specs.append(pl.BlockSpec((N_BRANCHES, bw, tn), lambda i, j: (0, 0, j)))
    return pl.pallas_call(
        _merge_kernel,
        grid=(t // tm, nj),
        in_specs=in_specs,
        out_specs=pl.BlockSpec((tm, tn), lambda i, j: (i, j)),
        out_shape=jax.ShapeDtypeStruct((t, d), _BF16),
        compiler_params=_cparams("parallel", "parallel"),
        name="merge",
    )(h2d, *outs, *([w_gate] * 4), *([b_gate] * 4), w_branch)


def _out_kernel(a_ref, w_ref, x_ref, g_ref, o_ref):
    o_ref[...] = x_ref[...] + g_ref[0] * jnp.dot(a_ref[...], w_ref[...], preferred_element_type=_F32)


def _out_proj(a2d, w, x2d, gate, seq, tm=1024, tn=512):
    t, d = x2d.shape
    tm = min(tm, seq)
    nst = seq // tm
    return pl.pallas_call(
        _out_kernel,
        grid=(t // tm, d // tn),
        in_specs=[pl.BlockSpec((tm, a2d.shape[1]), lambda i, j: (i, 0)),
                  pl.BlockSpec((a2d.shape[1], tn), lambda i, j: (0, j)),
                  pl.BlockSpec((tm, tn), lambda i, j: (i, j)),
                  pl.BlockSpec((1, 1, tn), lambda i, j: (i // nst, 0, j))],
        out_specs=pl.BlockSpec((tm, tn), lambda i, j: (i, j)),
        out_shape=jax.ShapeDtypeStruct((t, d), _F32),
        compiler_params=_cparams("parallel", "parallel"),
        name="out_proj",
    )(a2d, w, x2d, gate)


def _moe_kernel(te_ref, nu_ref, xs_ref, w1_ref, b1_ref, w2_ref, b2_ref, o_ref):
    i = pl.program_id(0)

    @pl.when(i < nu_ref[0])
    def _():
        u = jnp.dot(xs_ref[...], w1_ref[0], preferred_element_type=_F32) + b1_ref[0]
        f = u.shape[1] // 2
        x_glu = jnp.minimum(u[:, :f], SWIGLU_LIMIT)
        x_lin = jnp.clip(u[:, f:], -SWIGLU_LIMIT, SWIGLU_LIMIT)
        a = x_glu * jax.nn.sigmoid(SWIGLU_ALPHA * x_glu) * (x_lin + 1.0)
        o_ref[...] = jnp.dot(a.astype(_BF16), w2_ref[0], preferred_element_type=_F32) + b2_ref[0]

    @pl.when(i >= nu_ref[0])
    def _():
        o_ref[...] = jnp.zeros(o_ref.shape, o_ref.dtype)


def _moe_experts(tile_expert, n_used, xs, w1, b1, w2, b2):
    n_rows, d = xs.shape
    bm = _MOE_TILE
    f2 = w1.shape[2]
    grid_spec = pltpu.PrefetchScalarGridSpec(
        num_scalar_prefetch=2,
        grid=(n_rows // bm,),
        in_specs=[pl.BlockSpec((bm, d), lambda i, te, nu: (i, 0)),
                  pl.BlockSpec((1, d, f2), lambda i, te, nu: (te[i], 0, 0)),
                  pl.BlockSpec((1, 1, f2), lambda i, te, nu: (te[i], 0, 0)),
                  pl.BlockSpec((1, f2 // 2, d), lambda i, te, nu: (te[i], 0, 0)),
                  pl.BlockSpec((1, 1, d), lambda i, te, nu: (te[i], 0, 0))],
        out_specs=pl.BlockSpec((bm, d), lambda i, te, nu: (i, 0)),
    )
    return pl.pallas_call(
        _moe_kernel,
        grid_spec=grid_spec,
        out_shape=jax.ShapeDtypeStruct((n_rows, d), _F32),
        compiler_params=_cparams("arbitrary"),
        name="moe_experts",
    )(tile_expert, n_used, xs, w1, b1, w2, b2)


def _combine_kernel(x_ref, g_ref, w_ref, y_ref, o_ref):
    w = w_ref[...]
    acc = y_ref[0] * w[:, 0:1]
    for k in range(1, TOP_K):
        acc = acc + y_ref[k] * w[:, k:k + 1]
    o_ref[...] = x_ref[...] + g_ref[0] * acc


def _combine(x2d, gate, wts, ysel, seq, tm=512):
    t, d = x2d.shape
    tm = min(tm, seq)
    nst = seq // tm
    return pl.pallas_call(
        _combine_kernel,
        grid=(t // tm,),
        in_specs=[pl.BlockSpec((tm, d), lambda i: (i, 0)),
                  pl.BlockSpec((1, 1, d), lambda i: (i // nst, 0, 0)),
                  pl.BlockSpec((tm, TOP_K), lambda i: (i, 0)),
                  pl.BlockSpec((TOP_K, tm, d), lambda i: (0, i, 0))],
        out_specs=pl.BlockSpec((tm, d), lambda i: (i, 0)),
        out_shape=jax.ShapeDtypeStruct((t, d), _F32),
        compiler_params=_cparams("parallel"),
        name="moe_combine",
    )(x2d, gate, wts, ysel)


def _route(logits):
    t = logits.shape[0]
    tk = t * TOP_K
    bm = _MOE_TILE
    top_logit, top_idx = lax.top_k(logits, TOP_K)
    wts = jax.nn.softmax(top_logit, axis=-1)
    e_flat = top_idx.reshape(tk).astype(jnp.int32)
    onehot = (e_flat[:, None] == jnp.arange(N_EXPERTS, dtype=jnp.int32)[None, :]).astype(jnp.int32)
    csum = jnp.cumsum(onehot, axis=0)
    counts = csum[-1]
    rank = jnp.take_along_axis(csum, e_flat[:, None], axis=1)[:, 0] - 1
    padded = (counts + bm - 1) // bm * bm
    pad_end = jnp.cumsum(padded)
    pad_start = pad_end - padded
    dest = pad_start[e_flat] + rank
    n_rows = tk + N_EXPERTS * bm
    n_tiles = n_rows // bm
    row_token = jnp.zeros((n_rows,), jnp.int32).at[dest].set(jnp.arange(tk, dtype=jnp.int32) // TOP_K)
    tile_expert = jnp.minimum(
        jnp.searchsorted(pad_end, jnp.arange(n_tiles, dtype=jnp.int32) * bm, side="right"),
        N_EXPERTS - 1).astype(jnp.int32)
    n_used = (pad_end[-1:] // bm).astype(jnp.int32)
    return wts, dest, row_token, tile_expert, n_used


def _rope_tables(seq):
    inv_freq = ROPE_THETA ** (-jnp.arange(0, HEAD_DIM, 2, dtype=_F32) / HEAD_DIM)
    ang = jnp.arange(seq, dtype=_F32)[:, None] * inv_freq[None, :]
    cos, sin = jnp.cos(ang), jnp.sin(ang)
    return jnp.tile(cos, (1, 4)), jnp.concatenate([-sin, sin, -sin, sin], axis=1)


def _dup_heads(w):
    return jnp.concatenate([w[:, :HEAD_DIM], w[:, :HEAD_DIM], w[:, HEAD_DIM:], w[:, HEAD_DIM:]], axis=1)


def _tile_gain(g, reps, scale=1.0):
    return jnp.tile(g.astype(_F32) * scale, reps)


def _mixer(h, w_in, b_forget, qk_gain, sinks, lambda_qk, g_diff, w_branch, w_gate, b_gate,
           cos, sin, ones, biases, lambda_init):
    b, s, d = h.shape
    h2d = h.reshape(b * s, d)
    split_at = np.cumsum(IN_WIDTHS)[:-1].tolist()
    qa, ka, va, fa, qb, kb, vb, qc, kc, vc, qd, kd, vd = jnp.split(w_in, split_at, axis=1)
    scale = HEAD_DIM ** -0.5

    w_norm = jnp.concatenate([qa, ka], axis=1).astype(_BF16)
    g_norm = jnp.concatenate([_tile_gain(qk_gain[0, 0], 8, scale), _tile_gain(qk_gain[0, 1], 8)])
    w_rope = jnp.concatenate([qb, qc, kc, qd, kd], axis=1).astype(_BF16)
    g_rope = jnp.concatenate([_tile_gain(qk_gain[1, 0], 8, scale),
                              _tile_gain(qk_gain[2, 0], 8, scale), _tile_gain(qk_gain[2, 1], 8),
                              _tile_gain(qk_gain[3, 0], 8, scale), _tile_gain(qk_gain[3, 1], 8)])
    w_kb = _dup_heads(kb).astype(_BF16)
    g_kb = _tile_gain(qk_gain[1, 1], 4)
    w_plain = jnp.concatenate([va, vc, vd, _dup_heads(vb)], axis=1).astype(_BF16)
    w_fa = jnp.pad(fa, ((0, 0), (0, LANES - FOX_HEADS))).astype(_BF16)

    p_norm = _proj(h2d, w_norm, "norm", s, gain=g_norm[None], ones=ones).reshape(b, s, -1)
    p_rope = _proj(h2d, w_rope, "norm_rope", s, gain=g_rope[None], ones=ones, cos=cos,
                   sin=sin).reshape(b, s, -1)
    p_kb = _proj(h2d, w_kb, "norm_rope", s, gain=g_kb[None], ones=ones, cos=cos, sin=sin,
                 tn=256).reshape(b, s, -1)
    p_plain = _proj(h2d, w_plain, "plain", s, tn=256).reshape(b, s, -1)
    p_fa = _proj(h2d, w_fa, "plain", s, tn=LANES, out_dtype=_F32).reshape(b, s, -1)

    log_f = jax.nn.log_sigmoid(p_fa[..., :FOX_HEADS] + b_forget.astype(_F32))
    cum = jnp.cumsum(log_f, axis=1).transpose(0, 2, 1)
    cum = (cum[..., None], cum[:, :, None, :])

    lq1, lk1, lq2, lk2 = lambda_qk.astype(_F32)
    lam = jnp.exp(jnp.sum(lq1 * lk1)) - jnp.exp(jnp.sum(lq2 * lk2)) + lambda_init
    g_d = (g_diff.astype(_F32) * (1.0 - lambda_init)).reshape(1, LANES)

    b_causal, b_swa, b_dil = biases
    oa = _attention(p_norm, lambda p: p, p_norm, lambda p: 4 + p, p_plain, lambda p: p,
                    b_causal, 4, causal_full=True, cum=cum)
    ob = _attention(p_rope, lambda p: p, p_kb, lambda p: p // 2, p_plain, lambda p: 12 + p // 2,
                    b_swa, 4, sinks=sinks.astype(_F32))
    oc = _attention(p_rope, lambda p: 4 + p, p_rope, lambda p: 8 + p, p_plain, lambda p: 4 + p,
                    b_causal, 4, causal_full=True, lam=lam.reshape(1), g=g_d)
    od = _attention(p_rope, lambda p: 12 + p, p_rope, lambda p: 16 + p, p_plain, lambda p: 8 + p,
                    b_dil, 4)

    outs = [o.reshape(b * s, BRANCH_WIDTH) for o in (oa, ob, oc, od)]
    return _merge(h2d, outs, w_gate.astype(_BF16), b_gate.astype(_F32).reshape(1, -1),
                  w_branch.astype(_BF16), s)


def _moe(x2d, h, logits, gate, w1, b1, w2, b2, seq):
    t, d = x2d.shape
    wts, dest, row_token, tile_expert, n_used = _route(logits)
    xs = jnp.take(h.reshape(t, d), row_token, axis=0)
    w1s = jnp.concatenate([w1[:, :, 0::2], w1[:, :, 1::2]], axis=2).astype(_BF16)
    b1s = jnp.concatenate([b1[:, 0::2], b1[:, 1::2]], axis=1).astype(_F32)[:, None, :]
    ys = _moe_experts(tile_expert, n_used, xs, w1s, b1s, w2.astype(_BF16), b2.astype(_F32)[:, None, :])
    ysel = jnp.take(ys, dest.reshape(t, TOP_K).T.reshape(-1), axis=0).reshape(TOP_K, t, d)
    return _combine(x2d, gate, wts, ysel, seq)


def kernel(x, c, w_ada, b_ada, g_norm_mix, g_norm_ffn, w_in, b_forget, qk_gain, sinks, lambda_qk, g_diff, w_branch, w_gate, b_gate, w_out, w_router, b_router, w_mlp1, b_mlp1, w_mlp2, b_mlp2):
    b, s, d = x.shape
    depth = w_ada.shape[0]
    cos, sin = _rope_tables(s)
    r = np.arange(256)
    ones = jnp.asarray((r[:, None] // HEAD_DIM) == (r[None, :] // HEAD_DIM), _BF16)
    tq = min(_ATT_TILE, s)
    biases = (_bias_tables(tq, 0, _mult_causal),
              _bias_tables(tq, -(-(SWA_WINDOW - 1) // tq), _mult_swa),
              _bias_tables(tq, -(-DIL_PAIRS[-1][0] // tq), _mult_dilated))
    c8 = jnp.pad(c.astype(_F32), ((0, 8 - b), (0, 0)))
    x = x.astype(_F32)
    for l in range(depth):
        lambda_init = 0.8 - 0.6 * math.exp(-0.3 * l)
        mod = _ada(c8, w_ada[l], b_ada[l])[:b]
        sh1, sc1, g1, sh2, sc2, g2 = jnp.split(mod, 6, axis=-1)
        h = _norm_mod(x, g_norm_mix[l], sc1, sh1)
        merged = _mixer(h, w_in[l], b_forget[l], qk_gain[l], sinks[l], lambda_qk[l], g_diff[l],
                        w_branch[l], w_gate[l], b_gate[l], cos, sin, ones, biases, lambda_init)
        x2d = _out_proj(merged, w_out[l].astype(_BF16), x.reshape(b * s, d), g1.reshape(b, 1, d), s)
        wr = jnp.pad(w_router[l].astype(_F32), ((0, 0), (0, LANES - N_EXPERTS)))
        br = jnp.pad(b_router[l].astype(_F32), (0, LANES - N_EXPERTS)).reshape(1, LANES)
        h, logits = _norm_mod(x2d.reshape(b, s, d), g_norm_ffn[l], sc2, sh2, router=(wr, br))
        x2d = _moe(x2d, h, logits.reshape(b * s, LANES)[:, :N_EXPERTS], g2.reshape(b, 1, d),
                   w_mlp1[l], b_mlp1[l], w_mlp2[l], b_mlp2[l], s)
        x = x2d.reshape(b, s, d)
    return x
```
